```python
import jax, jax.numpy as jnp
from jax import lax
import numpy as np

D_MODEL = 4096
BATCH = 8
SEQ = 2048
DEPTH = 1
DEC_BATCH = 2
DEC_SEQ = 4096
PAST_LEN = 128

HEAD_DIM = 128
N_Q_HEADS = 16
N_KV_HEADS = 4
Q_GROUP = N_Q_HEADS // N_KV_HEADS
Q_WIDTH = N_Q_HEADS * HEAD_DIM
KV_WIDTH = N_KV_HEADS * HEAD_DIM
Q_BLOCK = 128
ROPE_THETA = 10000.0
GRID_W = 64
POOL_WINDOWS = (2, 4, 8, 16)
N_POOL_GROUPS = len(POOL_WINDOWS)
POOL_WIDTH = D_MODEL // 2
POOL_GROUP_C = POOL_WIDTH // N_POOL_GROUPS
POOL_OUT_C = D_MODEL // N_POOL_GROUPS
N_BRANCHES = 2
IN_WIDTH = Q_WIDTH + 2 * KV_WIDTH + POOL_WIDTH + N_BRANCHES * D_MODEL
D_FF = 11008
CONV_W = 3
EPS = 1e-6

kernel_name = "hybrid_gqa_pool_convglu_encoder"


def rmsnorm(x, g):
    xf = x.astype(jnp.float32)
    inv = lax.rsqrt(jnp.mean(xf * xf, axis=-1, keepdims=True) + EPS)
    return (xf * inv * g.astype(jnp.float32)).astype(x.dtype)


def axial_rope(x, s):
    rows = s // GRID_W
    half = HEAD_DIM // 2
    quarter = half // 2
    inv_freq = ROPE_THETA ** (-jnp.arange(0, half, 2, dtype=jnp.float32) / half)
    row = jnp.repeat(jnp.arange(rows, dtype=jnp.float32), GRID_W)
    col = jnp.tile(jnp.arange(GRID_W, dtype=jnp.float32), rows)
    ang_r = row[:, None] * inv_freq[None, :]
    ang_c = col[:, None] * inv_freq[None, :]
    ang = jnp.concatenate([ang_r, ang_r, ang_c, ang_c], axis=-1)
    cos = jnp.cos(ang)[None, :, None, :]
    sin = jnp.sin(ang)[None, :, None, :]
    xf = x.astype(jnp.float32)
    xr = xf.reshape(xf.shape[:-1] + (2, 2, quarter))
    rot = jnp.stack([-xr[..., 1, :], xr[..., 0, :]], axis=-2).reshape(xf.shape)
    return (xf * cos + rot * sin).astype(x.dtype)


def blocked_gqa(q, k, v):
    b, s = q.shape[0], q.shape[1]
    nb = s // Q_BLOCK
    scale = HEAD_DIM ** -0.5
    qb = q.reshape(b, nb, Q_BLOCK, N_KV_HEADS, Q_GROUP, HEAD_DIM).transpose(1, 0, 2, 3, 4, 5)

    def one_block(q_blk):
        sc = jnp.einsum('bqkgd,bskd->bkgqs', q_blk, k).astype(jnp.float32) * scale
        p = jax.nn.softmax(sc, axis=-1).astype(v.dtype)
        return jnp.einsum('bkgqs,bskd->bqkgd', p, v)

    o = lax.map(one_block, qb)
    return o.transpose(1, 0, 2, 3, 4, 5).reshape(b, s, Q_WIDTH)


def multiscale_pool(u):
    s = u.shape[1]
    cs = jnp.cumsum(u.astype(jnp.float32), axis=1)
    cs = jnp.concatenate([jnp.zeros_like(cs[:, :1]), cs], axis=1)
    t = jnp.arange(s)[:, None]
    w = jnp.array(POOL_WINDOWS, dtype=jnp.int32)[None, :]
    lo = jnp.clip(t - w // 2, 0, s)
    hi = jnp.clip(t + w - w // 2, 0, s)
    g_idx = jnp.arange(N_POOL_GROUPS)[None, :]
    window_sum = cs[:, hi, g_idx] - cs[:, lo, g_idx]
    count = (hi - lo).astype(jnp.float32)[None, :, :, None]
    return (window_sum / count - u.astype(jnp.float32)).astype(u.dtype)


def token_mixer(n, w_in, q_norm_g, k_norm_g, w_attn_proj, w_pool, pool_scale, w_out):
    b, s, _ = n.shape
    proj = n @ w_in
    o1 = Q_WIDTH
    o2 = o1 + KV_WIDTH
    o3 = o2 + KV_WIDTH
    o4 = o3 + POOL_WIDTH
    q = proj[..., :o1].reshape(b, s, N_Q_HEADS, HEAD_DIM)
    k = proj[..., o1:o2].reshape(b, s, N_KV_HEADS, HEAD_DIM)
    v = proj[..., o2:o3].reshape(b, s, N_KV_HEADS, HEAD_DIM)
    u = proj[..., o3:o4].reshape(b, s, N_POOL_GROUPS, POOL_GROUP_C)
    g_attn = proj[..., o4:o4 + D_MODEL]
    g_pool = proj[..., o4 + D_MODEL:]
    q = axial_rope(rmsnorm(q, q_norm_g), s)
    k = axial_rope(rmsnorm(k, k_norm_g), s)
    attn = blocked_gqa(q, k, v) @ w_attn_proj
    d = multiscale_pool(u)
    pool = jnp.einsum('bsgc,gce->bsge', d, w_pool).reshape(b, s, D_MODEL) * pool_scale
    merged = jax.nn.sigmoid(g_attn) * attn + jax.nn.sigmoid(g_pool) * pool
    return merged @ w_out


def channel_mixer(n, w_up, conv_w, conv_b, w_down):
    up = n @ w_up
    up_p = jnp.pad(up, ((0, 0), (1, 1), (0, 0)))
    c = up_p[:, :-2] * conv_w[0] + up_p[:, 1:-1] * conv_w[1] + up_p[:, 2:] * conv_w[2] + conv_b
    gate, val = c[..., :D_FF], c[..., D_FF:]
    return (jax.nn.silu(gate) * val) @ w_down


def setup_inputs(seed: int = 0) -> dict:
    key = jax.random.key(seed)
    ks = jax.random.split(key, 16)
    f = jnp.float32

    def nrm(k, shape, fan_in):
        return jax.random.normal(k, shape, f) * (fan_in ** -0.5)

    return {
        "x_prompt": jax.random.normal(ks[0], (BATCH, SEQ, D_MODEL), f),
        "x_sample": jax.random.normal(ks[1], (DEC_BATCH, DEC_SEQ, D_MODEL), f),
        "norm_mix_g": 1.0 + 0.02 * jax.random.normal(ks[2], (DEPTH, D_MODEL), f),
        "w_in": nrm(ks[3], (DEPTH, D_MODEL, IN_WIDTH), D_MODEL),
        "q_norm_g": 1.0 + 0.02 * jax.random.normal(ks[4], (DEPTH, HEAD_DIM), f),
        "k_norm_g": 1.0 + 0.02 * jax.random.normal(ks[5], (DEPTH, HEAD_DIM), f),
        "w_attn_proj": nrm(ks[6], (DEPTH, Q_WIDTH, D_MODEL), Q_WIDTH),
        "w_pool": nrm(ks[7], (DEPTH, N_POOL_GROUPS, POOL_GROUP_C, POOL_OUT_C), POOL_GROUP_C),
        "pool_scale": 1.0 + 0.02 * jax.random.normal(ks[8], (DEPTH, D_MODEL), f),
        "w_out": nrm(ks[9], (DEPTH, D_MODEL, D_MODEL), D_MODEL),
        "norm_ffn_g": 1.0 + 0.02 * jax.random.normal(ks[10], (DEPTH, D_MODEL), f),
        "w_up": nrm(ks[11], (DEPTH, D_MODEL, 2 * D_FF), D_MODEL),
        "conv_w": nrm(ks[12], (DEPTH, CONV_W, 2 * D_FF), CONV_W),
        "conv_b": 0.02 * jax.random.normal(ks[13], (DEPTH, 2 * D_FF), f),
        "w_down": nrm(ks[14], (DEPTH, D_FF, D_MODEL), D_FF),
        "norm_final_g": 1.0 + 0.02 * jax.random.normal(ks[15], (D_MODEL,), f),
    }


def reference(x_prompt, x_sample, norm_mix_g, w_in, q_norm_g, k_norm_g, w_attn_proj, w_pool,
              pool_scale, w_out, norm_ffn_g, w_up, conv_w, conv_b, w_down, norm_final_g):
    def trunk(x):
        h = x
        for l in range(DEPTH):
            h = h + token_mixer(rmsnorm(h, norm_mix_g[l]), w_in[l], q_norm_g[l], k_norm_g[l],
                                w_attn_proj[l], w_pool[l], pool_scale[l], w_out[l])
            h = h + channel_mixer(rmsnorm(h, norm_ffn_g[l]), w_up[l], conv_w[l], conv_b[l], w_down[l])
        return rmsnorm(h, norm_final_g)

    y_prompt = trunk(x_prompt)
    y_sample = trunk(x_sample)
    return (y_prompt, y_sample)
```

```python
import functools
import math

import jax
import jax.numpy as jnp
from jax import lax
from jax.experimental import pallas as pl
from jax.experimental.pallas import tpu as pltpu

EPS = 1e-6
HEAD_DIM = 128
ROPE_THETA = 10000.0
GRID_W = 64
POOL_WINDOWS = (2, 4, 8, 16)
POOL_HALO = 8
F32_SUBLANES = 8
BF16_SUBLANES = 16
LANES = 128
ROW_CHUNK = 64
V7X_VMEM_BYTES = 64 * 1024 * 1024
VMEM_RESERVE_BYTES = 6 * 1024 * 1024

BF16 = jnp.bfloat16
F32 = jnp.float32


def _vmem_limit(block_bytes, scratch_bytes=0, temp_bytes=0):
    want = 2 * block_bytes + scratch_bytes + temp_bytes + (4 << 20)
    return int(min(want, V7X_VMEM_BYTES - VMEM_RESERVE_BYTES))


def _pick_tile(dim, candidates):
    for c in candidates:
        if dim % c == 0:
            return c
    return dim


def _rms_scale(x, g):
    inv = lax.rsqrt(jnp.mean(x * x, axis=-1, keepdims=True) + EPS)
    return x * inv * g


def _rmsnorm_kernel(x_ref, g_ref, o_ref):
    o_ref[...] = _rms_scale(x_ref[...], g_ref[...]).astype(o_ref.dtype)


def _rmsnorm(x, g, out_dtype):
    m, d = x.shape
    tr = _pick_tile(m, (256, 128, 64, 32, 16))
    blk = tr * d * (4 + jnp.dtype(out_dtype).itemsize)
    return pl.pallas_call(
        _rmsnorm_kernel,
        grid=(m // tr,),
        in_specs=[pl.BlockSpec((tr, d), lambda i: (i, 0)),
                  pl.BlockSpec((1, d), lambda i: (0, 0))],
        out_specs=pl.BlockSpec((tr, d), lambda i: (i, 0)),
        out_shape=jax.ShapeDtypeStruct((m, d), out_dtype),
        compiler_params=pltpu.CompilerParams(
            dimension_semantics=("parallel",),
            vmem_limit_bytes=_vmem_limit(blk, temp_bytes=tr * d * 8)),
        name="rmsnorm",
    )(x, g.reshape(1, d))


def _matmul_kernel(a_ref, b_ref, o_ref):
    o_ref[...] = jnp.dot(a_ref[...], b_ref[...],
                         preferred_element_type=F32).astype(o_ref.dtype)


def _matmul_res_kernel(a_ref, b_ref, r_ref, o_ref):
    o_ref[...] = r_ref[...] + jnp.dot(a_ref[...], b_ref[...],
                                      preferred_element_type=F32)


def _matmul(a, b, out_dtype, residual=None, name="matmul"):
    m, k = a.shape
    _, n = b.shape
    tm = _pick_tile(m, (1024, 512, 256, 128))
    tn = _pick_tile(n, (1024, 512, 256, 128))
    osz = jnp.dtype(out_dtype).itemsize
    blk = tm * k * 2 + k * tn * 2 + tm * tn * osz
    in_specs = [pl.BlockSpec((tm, k), lambda i, j: (i, 0)),
                pl.BlockSpec((k, tn), lambda i, j: (0, j))]
    args = [a, b]
    body = _matmul_kernel
    if residual is not None:
        in_specs.append(pl.BlockSpec((tm, tn), lambda i, j: (i, j)))
        args.append(residual)
        blk += tm * tn * 4
        body = _matmul_res_kernel
    return pl.pallas_call(
        body,
        grid=(m // tm, n // tn),
        in_specs=in_specs,
        out_specs=pl.BlockSpec((tm, tn), lambda i, j: (i, j)),
        out_shape=jax.ShapeDtypeStruct((m, n), out_dtype),
        compiler_params=pltpu.CompilerParams(
            dimension_semantics=("parallel", "parallel"),
            vmem_limit_bytes=_vmem_limit(blk, temp_bytes=tm * tn * 4)),
        name=name,
    )(*args)


def _rope_tables(s):
    rows = s // GRID_W
    half = HEAD_DIM // 2
    quarter = half // 2
    inv_freq = ROPE_THETA ** (-jnp.arange(0, half, 2, dtype=F32) / half)
    row = jnp.repeat(jnp.arange(rows, dtype=F32), GRID_W)
    col = jnp.tile(jnp.arange(GRID_W, dtype=F32), rows)
    ang_r = row[:, None] * inv_freq[None, :]
    ang_c = col[:, None] * inv_freq[None, :]
    ang = jnp.concatenate([ang_r, ang_r, ang_c, ang_c], axis=-1)
    cos, sin = jnp.cos(ang), jnp.sin(ang)
    first = (jnp.arange(HEAD_DIM) % half) < quarter
    sin_a = jnp.where(first[None, :], -sin, 0.0)
    sin_b = jnp.where(first[None, :], 0.0, sin)
    return cos, sin_a, sin_b


def _norm_rope(x, g, cos, sin_a, sin_b):
    quarter = HEAD_DIM // 4
    xn = _rms_scale(x, g)
    return (xn * cos + pltpu.roll(xn, HEAD_DIM - quarter, 1) * sin_a
            + pltpu.roll(xn, quarter, 1) * sin_b)


def _attn_kernel(q_ref, k_ref, v_ref, cq_ref, saq_ref, sbq_ref, ck_ref, sak_ref, sbk_ref,
                 qg_ref, kg_ref, o_ref, kt_ref, *, tq, tk, seq, group, scale):
    qi = pl.program_id(2)

    @pl.when(qi == 0)
    def _():
        k = _norm_rope(k_ref[...].astype(F32), kg_ref[...], ck_ref[...], sak_ref[...],
                       sbk_ref[...])
        kt_ref[...] = k.T.astype(BF16)

    heads = []
    for g in range(group):
        x = q_ref[:, g * HEAD_DIM:(g + 1) * HEAD_DIM].astype(F32)
        x = _norm_rope(x, qg_ref[...], cq_ref[...], saq_ref[...], sbq_ref[...])
        heads.append((x * scale).astype(BF16))
    q = jnp.concatenate(heads, axis=0)

    rows = group * tq
    m = jnp.full((rows, 1), -jnp.inf, F32)
    l = jnp.zeros((rows, 1), F32)
    acc = jnp.zeros((rows, HEAD_DIM), F32)
    for c in range(seq // tk):
        s = jnp.dot(q, kt_ref[:, c * tk:(c + 1) * tk], preferred_element_type=F32)
        m_new = jnp.maximum(m, jnp.max(s, axis=-1, keepdims=True))
        alpha = jnp.exp(m - m_new)
        p = jnp.exp(s - m_new)
        l = alpha * l + jnp.sum(p, axis=-1, keepdims=True)
        acc = alpha * acc + jnp.dot(p.astype(BF16), v_ref[c * tk:(c + 1) * tk, :],
                                    preferred_element_type=F32)
        m = m_new
    o = acc / l
    for g in range(group):
        o_ref[:, g * HEAD_DIM:(g + 1) * HEAD_DIM] = o[g * tq:(g + 1) * tq].astype(o_ref.dtype)


def _attention(proj, batch, seq, n_q_heads, n_kv_heads, q_norm_g, k_norm_g):
    m = batch * seq
    group = n_q_heads // n_kv_heads
    q_width = n_q_heads * HEAD_DIM
    kv_width = n_kv_heads * HEAD_DIM
    tq = min(128, seq)
    tk = min(512, seq)
    nq = seq // tq
    gw = group * HEAD_DIM
    k_blk0 = q_width // HEAD_DIM
    v_blk0 = (q_width + kv_width) // HEAD_DIM
    cos, sin_a, sin_b = _rope_tables(seq)
    qtab = pl.BlockSpec((tq, HEAD_DIM), lambda b, h, i: (i, 0))
    ktab = pl.BlockSpec((seq, HEAD_DIM), lambda b, h, i: (0, 0))
    gspec = pl.BlockSpec((1, HEAD_DIM), lambda b, h, i: (0, 0))
    blk = (tq * gw * 2 * 2 + 2 * seq * HEAD_DIM * 2 + 3 * tq * HEAD_DIM * 4
           + 3 * seq * HEAD_DIM * 4)
    return pl.pallas_call(
        functools.partial(_attn_kernel, tq=tq, tk=tk, seq=seq, group=group,
                          scale=HEAD_DIM ** -0.5),
        grid=(batch, n_kv_heads, nq),
        in_specs=[
            pl.BlockSpec((tq, gw), lambda b, h, i: (b * nq + i, h)),
            pl.BlockSpec((seq, HEAD_DIM), lambda b, h, i: (b, k_blk0 + h)),
            pl.BlockSpec((seq, HEAD_DIM), lambda b, h, i: (b, v_blk0 + h)),
            qtab, qtab, qtab, ktab, ktab, ktab, gspec, gspec,
        ],
        out_specs=pl.BlockSpec((tq, gw), lambda b, h, i: (b * nq + i, h)),
        out_shape=jax.ShapeDtypeStruct((m, q_width), BF16),
        scratch_shapes=[pltpu.VMEM((HEAD_DIM, seq), BF16)],
        compiler_params=pltpu.CompilerParams(
            dimension_semantics=("parallel", "parallel", "arbitrary"),
            vmem_limit_bytes=_vmem_limit(blk, scratch_bytes=seq * HEAD_DIM * 2,
                                         temp_bytes=seq * HEAD_DIM * 16 + (8 << 20))),
        name="attention",
    )(proj, proj, proj, cos, sin_a, sin_b, cos, sin_a, sin_b,
      q_norm_g.reshape(1, HEAD_DIM), k_norm_g.reshape(1, HEAD_DIM))


def _merge_kernel(ao_ref, wap_ref, u_ref, uprev_ref, unext_ref, wp_ref, ps_ref, ga_ref, gp_ref,
                  o_ref, uext_ref, d_ref, *, tm, seq, tiles_per_seq):
    i = pl.program_id(0)
    g = pl.program_id(1)
    ti = i % tiles_per_seq
    keep_prev = (ti != 0).astype(F32)
    keep_next = (ti != tiles_per_seq - 1).astype(F32)
    uext_ref[0:POOL_HALO, :] = uprev_ref[...].astype(F32)[BF16_SUBLANES - POOL_HALO:] * keep_prev
    uext_ref[POOL_HALO:POOL_HALO + tm, :] = u_ref[...].astype(F32)
    uext_ref[POOL_HALO + tm:, :] = unext_ref[...].astype(F32)[:POOL_HALO] * keep_next

    t = ti * tm + lax.broadcasted_iota(jnp.int32, (tm, 1), 0)
    for gg, w in enumerate(POOL_WINDOWS):
        @pl.when(g == gg)
        def _(w=w):
            lo_off = -(w // 2)
            hi_off = w - w // 2
            wsum = uext_ref[POOL_HALO + lo_off:POOL_HALO + lo_off + tm, :]
            for off in range(lo_off + 1, hi_off):
                wsum = wsum + uext_ref[POOL_HALO + off:POOL_HALO + off + tm, :]
            count = (jnp.minimum(t + hi_off, seq) - jnp.maximum(t + lo_off, 0)).astype(F32)
            d = wsum / count - uext_ref[POOL_HALO:POOL_HALO + tm, :]
            d_ref[...] = d.astype(BF16)

    attn = jnp.dot(ao_ref[...], wap_ref[...], preferred_element_type=F32)
    pool = jnp.dot(d_ref[...], wp_ref[0], preferred_element_type=F32) * ps_ref[...]
    merged = (jax.nn.sigmoid(ga_ref[...].astype(F32)) * attn
              + jax.nn.sigmoid(gp_ref[...].astype(F32)) * pool)
    o_ref[...] = merged.astype(o_ref.dtype)


def _merge(attn_o, proj, w_ap, w_pool, pool_scale, seq, u_off, ga_off, gp_off):
    m, q_width = attn_o.shape
    n_groups, gc, oc = w_pool.shape
    d_model = n_groups * oc
    assert n_groups == len(POOL_WINDOWS)
    tm = _pick_tile(seq, (512, 256, 128))
    tiles_per_seq = seq // tm
    u_blk0, ga_blk0, gp_blk0 = u_off // gc, ga_off // oc, gp_off // oc
    assert u_off % gc == 0 and ga_off % oc == 0 and gp_off % oc == 0
    hb = tm // BF16_SUBLANES
    n_hblk = m // BF16_SUBLANES
    blk = (tm * q_width * 2 + q_width * oc * 2 + tm * gc * 2 + 2 * BF16_SUBLANES * gc * 2
           + gc * oc * 2 + oc * 4 + 3 * tm * oc * 2)
    return pl.pallas_call(
        functools.partial(_merge_kernel, tm=tm, seq=seq, tiles_per_seq=tiles_per_seq),
        grid=(m // tm, n_groups),
        in_specs=[
            pl.BlockSpec((tm, q_width), lambda i, g: (i, 0)),
            pl.BlockSpec((q_width, oc), lambda i, g: (0, g)),
            pl.BlockSpec((tm, gc), lambda i, g: (i, u_blk0 + g)),
            pl.BlockSpec((BF16_SUBLANES, gc),
                         lambda i, g: (jnp.maximum(i * hb - 1, 0), u_blk0 + g)),
            pl.BlockSpec((BF16_SUBLANES, gc),
                         lambda i, g: (jnp.minimum((i + 1) * hb, n_hblk - 1), u_blk0 + g)),
            pl.BlockSpec((1, gc, oc), lambda i, g: (g, 0, 0)),
            pl.BlockSpec((1, oc), lambda i, g: (0, g)),
            pl.BlockSpec((tm, oc), lambda i, g: (i, ga_blk0 + g)),
            pl.BlockSpec((tm, oc), lambda i, g: (i, gp_blk0 + g)),
        ],
        out_specs=pl.BlockSpec((tm, oc), lambda i, g: (i, g)),
        out_shape=jax.ShapeDtypeStruct((m, d_model), BF16),
        scratch_shapes=[pltpu.VMEM((tm + 2 * POOL_HALO, gc), F32),
                        pltpu.VMEM((tm, gc), BF16)],
        compiler_params=pltpu.CompilerParams(
            dimension_semantics=("parallel", "arbitrary"),
            vmem_limit_bytes=_vmem_limit(blk, scratch_bytes=(tm + 16) * gc * 4 + tm * gc * 2,
                                         temp_bytes=6 * tm * oc * 4)),
        name="merge",
    )(attn_o, w_ap, proj, proj, proj, w_pool, pool_scale.reshape(1, d_model), proj, proj)


def _ffn_kernel(h_ref, hprev_ref, hnext_ref, gn_ref, wg_ref, wv_ref, cwg_ref, cwv_ref,
                cbg_ref, cbv_ref, wd_ref, gf_ref, o_ref, next_ref, *, tm, tiles_per_seq):
    i = pl.program_id(0)
    j = pl.program_id(1)
    ti = i % tiles_per_seq
    halo = BF16_SUBLANES

    @pl.when(j == 0)
    def _():
        edge = jnp.concatenate([hprev_ref[...], hnext_ref[...]], axis=0)
        next_ref[0:halo, :] = _rms_scale(edge, gn_ref[...]).astype(BF16)

        def norm_rows(c, carry):
            r = pl.multiple_of(c * ROW_CHUNK, ROW_CHUNK)
            next_ref[pl.ds(halo + r, ROW_CHUNK), :] = _rms_scale(
                h_ref[pl.ds(r, ROW_CHUNK), :], gn_ref[...]).astype(BF16)
            o_ref[pl.ds(r, ROW_CHUNK), :] = jnp.zeros((ROW_CHUNK, o_ref.shape[1]), F32)
            return carry
        lax.fori_loop(0, tm // ROW_CHUNK, norm_rows, 0)

    keep_prev = (ti != 0).astype(F32)
    keep_next = (ti != tiles_per_seq - 1).astype(F32)
    row = lax.broadcasted_iota(jnp.int32, (tm, 1), 0)

    def conv(w_ref, cw_ref, cb_ref):
        up = jnp.dot(next_ref[...], w_ref[...], preferred_element_type=F32)
        main = up[halo:]
        before = up[F32_SUBLANES - 1:F32_SUBLANES] * keep_prev
        after = up[F32_SUBLANES:F32_SUBLANES + 1] * keep_next
        dn = jnp.where(row == 0, before, pltpu.roll(main, 1, 0))
        nx = jnp.where(row == tm - 1, after, pltpu.roll(main, tm - 1, 0))
        return dn * cw_ref[0:1, :] + main * cw_ref[1:2, :] + nx * cw_ref[2:3, :] + cb_ref[...]

    gate = conv(wg_ref, cwg_ref, cbg_ref)
    val = conv(wv_ref, cwv_ref, cbv_ref)
    act = (jax.nn.silu(gate) * val).astype(BF16)
    o_ref[...] += jnp.dot(act, wd_ref[...], preferred_element_type=F32)

    @pl.when(j == pl.num_programs(1) - 1)
    def _():
        def final_rows(c, carry):
            r = pl.multiple_of(c * ROW_CHUNK, ROW_CHUNK)
            y = h_ref[pl.ds(r, ROW_CHUNK), :] + o_ref[pl.ds(r, ROW_CHUNK), :]
            o_ref[pl.ds(r, ROW_CHUNK), :] = _rms_scale(y, gf_ref[...])
            return carry
        lax.fori_loop(0, tm // ROW_CHUNK, final_rows, 0)


def _ffn(h, norm_g, w_up, conv_w, conv_b, w_down, final_g, seq):
    m, d = h.shape
    d_ff = w_down.shape[0]
    tc = _pick_tile(d_ff, (256, 128))
    nj = d_ff // tc
    tm = _pick_tile(seq, (512, 256, 128))
    tiles_per_seq = seq // tm
    hb = tm // F32_SUBLANES
    n_hblk = m // F32_SUBLANES
    blk = (tm * d * 4 + 2 * F32_SUBLANES * d * 4 + 2 * d * tc * 2 + tc * d * 2 + tm * d * 4)
    return pl.pallas_call(
        functools.partial(_ffn_kernel, tm=tm, tiles_per_seq=tiles_per_seq),
        grid=(m // tm, nj),
        in_specs=[
            pl.BlockSpec((tm, d), lambda i, j: (i, 0)),
            pl.BlockSpec((F32_SUBLANES, d), lambda i, j: (jnp.maximum(i * hb - 1, 0), 0)),
            pl.BlockSpec((F32_SUBLANES, d),
                         lambda i, j: (jnp.minimum((i + 1) * hb, n_hblk - 1), 0)),
            pl.BlockSpec((1, d), lambda i, j: (0, 0)),
            pl.BlockSpec((d, tc), lambda i, j: (0, j)),
            pl.BlockSpec((d, tc), lambda i, j: (0, nj + j)),
            pl.BlockSpec((3, tc), lambda i, j: (0, j)),
            pl.BlockSpec((3, tc), lambda i, j: (0, nj + j)),
            pl.BlockSpec((1, tc), lambda i, j: (0, j)),
            pl.BlockSpec((1, tc), lambda i, j: (0, nj + j)),
            pl.BlockSpec((tc, d), lambda i, j: (j, 0)),
            pl.BlockSpec((1, d), lambda i, j: (0, 0)),
        ],
        out_specs=pl.BlockSpec((tm, d), lambda i, j: (i, 0)),
        out_shape=jax.ShapeDtypeStruct((m, d), F32),
        scratch_shapes=[pltpu.VMEM((tm + BF16_SUBLANES, d), BF16)],
        compiler_params=pltpu.CompilerParams(
            dimension_semantics=("parallel", "arbitrary"),
            vmem_limit_bytes=_vmem_limit(blk, scratch_bytes=(tm + 16) * d * 2,
                                         temp_bytes=8 * tm * tc * 4)),
        name="ffn",
    )(h, h, h, norm_g.reshape(1, d), w_up, w_up, conv_w, conv_w,
      conv_b.reshape(1, -1), conv_b.reshape(1, -1), w_down, final_g.reshape(1, d))


def _trunk(x, layers, final_g):
    batch, seq, d_model = x.shape
    h = x.reshape(batch * seq, d_model)
    n_layers = len(layers)
    for li, p in enumerate(layers):
        q_width = p["w_attn_proj"].shape[0]
        n_groups, gc, _ = p["w_pool"].shape
        pool_width = n_groups * gc
        in_width = p["w_in"].shape[1]
        kv_width = (in_width - q_width - pool_width - 2 * d_model) // 2
        n_q_heads, n_kv_heads = q_width // HEAD_DIM, kv_width // HEAD_DIM
        u_off = q_width + 2 * kv_width
        ga_off = u_off + pool_width
        gp_off = ga_off + d_model

        n = _rmsnorm(h, p["norm_mix_g"], BF16)
        proj = _matmul(n, p["w_in"], BF16, name="in_proj")
        attn_o = _attention(proj, batch, seq, n_q_heads, n_kv_heads, p["q_norm_g"], p["k_norm_g"])
        merged = _merge(attn_o, proj, p["w_attn_proj"], p["w_pool"], p["pool_scale"], seq,
                        u_off, ga_off, gp_off)
        h = _matmul(merged, p["w_out"], F32, residual=h, name="out_proj")
        assert li == n_layers - 1, "only the last layer's FFN applies the final norm"
        h = _ffn(h, p["norm_ffn_g"], p["w_up"], p["conv_w"], p["conv_b"], p["w_down"],
                 final_g, seq)
    return h.reshape(batch, seq, d_model)


def kernel(x_prompt, x_sample, norm_mix_g, w_in, q_norm_g, k_norm_g, w_attn_proj, w_pool,
           pool_scale, w_out, norm_ffn_g, w_up, conv_w, conv_b, w_down, norm_final_g):
    depth = w_in.shape[0]
    assert depth == 1, "the fused FFN epilogue assumes a single layer"
    layers = []
    for l in range(depth):
        layers.append({
            "norm_mix_g": norm_mix_g[l], "w_in": w_in[l].astype(BF16),
            "q_norm_g": q_norm_g[l], "k_norm_g": k_norm_g[l],
            "w_attn_proj": w_attn_proj[l].astype(BF16), "w_pool": w_pool[l].astype(BF16),
            "pool_scale": pool_scale[l], "w_out": w_out[l].astype(BF16),
            "norm_ffn_g": norm_ffn_g[l], "w_up": w_up[l].astype(BF16),
            "conv_w": conv_w[l], "conv_b": conv_b[l], "w_down": w_down[l].astype(BF16),
        })
    return (_trunk(x_prompt, layers, norm_final_g), _trunk(x_sample, layers, norm_final_g))
```

```python
import functools

import jax
import jax.numpy as jnp
from jax import lax
from jax.experimental import pallas as pl
from jax.experimental.pallas import tpu as pltpu

EPS = 1e-6
HEAD_DIM = 128
ROPE_THETA = 10000.0
GRID_W = 64
POOL_WINDOWS = (2, 4, 8, 16)
POOL_HALO = 8
F32_SUBLANES = 8
BF16_SUBLANES = 16
ROW_CHUNK = 32
FFN_CHUNK = 512
FFN_SUB = 256
V7X_VMEM_BYTES = 64 * 1024 * 1024
VMEM_RESERVE_BYTES = 2 * 1024 * 1024

BF16 = jnp.bfloat16
F32 = jnp.float32


def _vmem_limit(block_bytes, scratch_bytes=0, temp_bytes=0):
    want = 2 * block_bytes + scratch_bytes + temp_bytes + (4 << 20)
    return int(min(want, V7X_VMEM_BYTES - VMEM_RESERVE_BYTES))


def _pick_tile(dim, candidates):
    for c in candidates:
        if dim % c == 0:
            return c
    return dim


def _rms_scale(x, g):
    inv = lax.rsqrt(jnp.mean(x * x, axis=-1, keepdims=True) + EPS)
    return x * inv * g


def _rmsnorm_kernel(x_ref, g_ref, o_ref):
    o_ref[...] = _rms_scale(x_ref[...], g_ref[...]).astype(o_ref.dtype)


def _rmsnorm(x, g, out_dtype):
    m, d = x.shape
    tr = _pick_tile(m, (256, 128, 64, 32, 16))
    blk = tr * d * (4 + jnp.dtype(out_dtype).itemsize)
    return pl.pallas_call(
        _rmsnorm_kernel,
        grid=(m // tr,),
        in_specs=[pl.BlockSpec((tr, d), lambda i: (i, 0)),
                  pl.BlockSpec((1, d), lambda i: (0, 0))],
        out_specs=pl.BlockSpec((tr, d), lambda i: (i, 0)),
        out_shape=jax.ShapeDtypeStruct((m, d), out_dtype),
        compiler_params=pltpu.CompilerParams(
            dimension_semantics=("parallel",),
            vmem_limit_bytes=_vmem_limit(blk, temp_bytes=tr * d * 8)),
        name="rmsnorm",
    )(x, g.reshape(1, d))


def _matmul_kernel(a_ref, b_ref, o_ref):
    o_ref[...] = jnp.dot(a_ref[...], b_ref[...],
                         preferred_element_type=F32).astype(o_ref.dtype)


def _matmul_res_kernel(a_ref, b_ref, r_ref, o_ref):
    o_ref[...] = r_ref[...] + jnp.dot(a_ref[...], b_ref[...],
                                      preferred_element_type=F32)


def _matmul(a, b, out_dtype, residual=None, name="matmul"):
    m, k = a.shape
    _, n = b.shape
    tm = _pick_tile(m, (1024, 512, 256, 128))
    tn = _pick_tile(n, (1024, 512, 256, 128))
    osz = jnp.dtype(out_dtype).itemsize
    blk = tm * k * 2 + k * tn * 2 + tm * tn * osz
    in_specs = [pl.BlockSpec((tm, k), lambda i, j: (i, 0)),
                pl.BlockSpec((k, tn), lambda i, j: (0, j))]
    args = [a, b]
    body = _matmul_kernel
    if residual is not None:
        in_specs.append(pl.BlockSpec((tm, tn), lambda i, j: (i, j)))
        args.append(residual)
        blk += tm * tn * 4
        body = _matmul_res_kernel
    return pl.pallas_call(
        body,
        grid=(m // tm, n // tn),
        in_specs=in_specs,
        out_specs=pl.BlockSpec((tm, tn), lambda i, j: (i, j)),
        out_shape=jax.ShapeDtypeStruct((m, n), out_dtype),
        compiler_params=pltpu.CompilerParams(
            dimension_semantics=("parallel", "parallel"),
            vmem_limit_bytes=_vmem_limit(blk, temp_bytes=tm * tn * 4)),
        name=name,
    )(*args)


def _rope_tables(s):
    rows = s // GRID_W
    half = HEAD_DIM // 2
    quarter = half // 2
    inv_freq = ROPE_THETA ** (-jnp.arange(0, half, 2, dtype=F32) / half)
    row = jnp.repeat(jnp.arange(rows, dtype=F32), GRID_W)
    col = jnp.tile(jnp.arange(GRID_W, dtype=F32), rows)
    ang_r = row[:, None] * inv_freq[None, :]
    ang_c = col[:, None] * inv_freq[None, :]
    ang = jnp.concatenate([ang_r, ang_r, ang_c, ang_c], axis=-1)
    cos, sin = jnp.cos(ang), jnp.sin(ang)
    first = (jnp.arange(HEAD_DIM) % half) < quarter
    sin_a = jnp.where(first[None, :], -sin, 0.0)
    sin_b = jnp.where(first[None, :], 0.0, sin)
    return cos, sin_a, sin_b


def _norm_rope(x, g, cos, sin_a, sin_b):
    quarter = HEAD_DIM // 4
    xn = _rms_scale(x, g)
    return (xn * cos + pltpu.roll(xn, HEAD_DIM - quarter, 1) * sin_a
            + pltpu.roll(xn, quarter, 1) * sin_b)


def _qk_prep_kernel(x_ref, cos_ref, sa_ref, sb_ref, qg_ref, kg_ref, q_ref, kt_ref, *,
                    n_q_heads, n_kv_heads, scale):
    cos, sa, sb = cos_ref[...], sa_ref[...], sb_ref[...]
    for h in range(n_q_heads):
        x = x_ref[:, h * HEAD_DIM:(h + 1) * HEAD_DIM].astype(F32)
        q_ref[:, h * HEAD_DIM:(h + 1) * HEAD_DIM] = (
            _norm_rope(x, qg_ref[...], cos, sa, sb) * scale).astype(BF16)
    for h in range(n_kv_heads):
        c0 = (n_q_heads + h) * HEAD_DIM
        x = x_ref[:, c0:c0 + HEAD_DIM].astype(F32)
        kt_ref[h] = _norm_rope(x, kg_ref[...], cos, sa, sb).T.astype(BF16)


def _qk_prep(proj, batch, seq, n_q_heads, n_kv_heads, q_norm_g, k_norm_g):
    m = batch * seq
    q_width = n_q_heads * HEAD_DIM
    qk_width = (n_q_heads + n_kv_heads) * HEAD_DIM
    tr = _pick_tile(seq, (256, 128))
    nt = seq // tr
    cos, sin_a, sin_b = _rope_tables(seq)
    tab = pl.BlockSpec((tr, HEAD_DIM), lambda i: (i % nt, 0))
    gspec = pl.BlockSpec((1, HEAD_DIM), lambda i: (0, 0))
    blk = tr * qk_width * 2 * 2 + 3 * tr * HEAD_DIM * 4
    return pl.pallas_call(
        functools.partial(_qk_prep_kernel, n_q_heads=n_q_heads, n_kv_heads=n_kv_heads,
                          scale=HEAD_DIM ** -0.5),
        grid=(m // tr,),
        in_specs=[pl.BlockSpec((tr, qk_width), lambda i: (i, 0)), tab, tab, tab, gspec, gspec],
        out_specs=[pl.BlockSpec((tr, q_width), lambda i: (i, 0)),
                   pl.BlockSpec((n_kv_heads, HEAD_DIM, tr), lambda i: (i // nt, 0, i % nt))],
        out_shape=[jax.ShapeDtypeStruct((m, q_width), BF16),
                   jax.ShapeDtypeStruct((batch * n_kv_heads, HEAD_DIM, seq), BF16)],
        compiler_params=pltpu.CompilerParams(
            dimension_semantics=("parallel",),
            vmem_limit_bytes=_vmem_limit(blk, temp_bytes=16 << 20)),
        name="qk_prep",
    )(proj, cos, sin_a, sin_b, q_norm_g.reshape(1, HEAD_DIM), k_norm_g.reshape(1, HEAD_DIM))


def _attn_kernel(q_ref, kt_ref, v_ref, o_ref, *, tq, tk, seq, group):
    q = jnp.concatenate([q_ref[:, g * HEAD_DIM:(g + 1) * HEAD_DIM] for g in range(group)],
                        axis=0)
    rows = group * tq
    m = jnp.full((rows, 1), -jnp.inf, F32)
    l = jnp.zeros((rows, 1), F32)
    acc = jnp.zeros((rows, HEAD_DIM), F32)
    for c in range(seq // tk):
        s = jnp.dot(q, kt_ref[0, :, c * tk:(c + 1) * tk], preferred_element_type=F32)
        m_new = jnp.maximum(m, jnp.max(s, axis=-1, keepdims=True))
        alpha = jnp.exp(m - m_new)
        p = jnp.exp(s - m_new)
        l = alpha * l + jnp.sum(p, axis=-1, keepdims=True)
        acc = alpha * acc + jnp.dot(p.astype(BF16), v_ref[c * tk:(c + 1) * tk, :],
                                    preferred_element_type=F32)
        m = m_new
    o = acc / l
    for g in range(group):
        o_ref[:, g * HEAD_DIM:(g + 1) * HEAD_DIM] = o[g * tq:(g + 1) * tq].astype(o_ref.dtype)


def _attention(q, kt, proj, batch, seq, n_q_heads, n_kv_heads, v_off):
    m = batch * seq
    group = n_q_heads // n_kv_heads
    q_width = n_q_heads * HEAD_DIM
    tq = _pick_tile(seq, (256, 128))
    tk = min(512, seq)
    nq = seq // tq
    gw = group * HEAD_DIM
    v_blk0 = v_off // HEAD_DIM
    blk = tq * gw * 2 * 2 + 2 * seq * HEAD_DIM * 2
    return pl.pallas_call(
        functools.partial(_attn_kernel, tq=tq, tk=tk, seq=seq, group=group),
        grid=(batch, n_kv_heads, nq),
        in_specs=[
            pl.BlockSpec((tq, gw), lambda b, h, i: (b * nq + i, h)),
            pl.BlockSpec((1, HEAD_DIM, seq), lambda b, h, i: (b * n_kv_heads + h, 0, 0)),
            pl.BlockSpec((seq, HEAD_DIM), lambda b, h, i: (b, v_blk0 + h)),
        ],
        out_specs=pl.BlockSpec((tq, gw), lambda b, h, i: (b * nq + i, h)),
        out_shape=jax.ShapeDtypeStruct((m, q_width), BF16),
        compiler_params=pltpu.CompilerParams(
            dimension_semantics=("parallel", "parallel", "parallel"),
            vmem_limit_bytes=_vmem_limit(blk, temp_bytes=10 * group * tq * tk * 4)),
        name="attention",
    )(q, kt, proj)


def _merge_kernel(ao_ref, wap_ref, u_ref, uprev_ref, unext_ref, wp_ref, ps_ref, ga_ref, gp_ref,
                  o_ref, uext_ref, d_ref, *, tm, seq, tiles_per_seq):
    i = pl.program_id(0)
    g = pl.program_id(1)
    ti = i % tiles_per_seq
    keep_prev = (ti != 0).astype(F32)
    keep_next = (ti != tiles_per_seq - 1).astype(F32)
    uext_ref[0:POOL_HALO, :] = uprev_ref[...].astype(F32)[BF16_SUBLANES - POOL_HALO:] * keep_prev
    uext_ref[POOL_HALO:POOL_HALO + tm, :] = u_ref[...].astype(F32)
    uext_ref[POOL_HALO + tm:, :] = unext_ref[...].astype(F32)[:POOL_HALO] * keep_next

    t = ti * tm + lax.broadcasted_iota(jnp.int32, (tm, 1), 0)
    for gg, w in enumerate(POOL_WINDOWS):
        @pl.when(g == gg)
        def _(w=w):
            lo_off = -(w // 2)
            hi_off = w - w // 2
            wsum = uext_ref[POOL_HALO + lo_off:POOL_HALO + lo_off + tm, :]
            for off in range(lo_off + 1, hi_off):
                wsum = wsum + uext_ref[POOL_HALO + off:POOL_HALO + off + tm, :]
            count = (jnp.minimum(t + hi_off, seq) - jnp.maximum(t + lo_off, 0)).astype(F32)
            d = wsum / count - uext_ref[POOL_HALO:POOL_HALO + tm, :]
            d_ref[...] = d.astype(BF16)

    attn = jnp.dot(ao_ref[...], wap_ref[...], preferred_element_type=F32)
    pool = jnp.dot(d_ref[...], wp_ref[0], preferred_element_type=F32) * ps_ref[...]
    merged = (jax.nn.sigmoid(ga_ref[...].astype(F32)) * attn
              + jax.nn.sigmoid(gp_ref[...].astype(F32)) * pool)
    o_ref[...] = merged.astype(o_ref.dtype)


def _merge(attn_o, proj, w_ap, w_pool, pool_scale, seq, u_off, ga_off, gp_off):
    m, q_width = attn_o.shape
    n_groups, gc, oc = w_pool.shape
    d_model = n_groups * oc
    assert n_groups == len(POOL_WINDOWS)
    tm = _pick_tile(seq, (512, 256, 128))
    tiles_per_seq = seq // tm
    u_blk0, ga_blk0, gp_blk0 = u_off // gc, ga_off // oc, gp_off // oc
    assert u_off % gc == 0 and ga_off % oc == 0 and gp_off % oc == 0
    hb = tm // BF16_SUBLANES
    n_hblk = m // BF16_SUBLANES
    blk = (tm * q_width * 2 + q_width * oc * 2 + tm * gc * 2 + 2 * BF16_SUBLANES * gc * 2
           + gc * oc * 2 + oc * 4 + 3 * tm * oc * 2)
    return pl.pallas_call(
        functools.partial(_merge_kernel, tm=tm, seq=seq, tiles_per_seq=tiles_per_seq),
        grid=(m // tm, n_groups),
        in_specs=[
            pl.BlockSpec((tm, q_width), lambda i, g: (i, 0)),
            pl.BlockSpec((q_width, oc), lambda i, g: (0, g)),
            pl.BlockSpec((tm, gc), lambda i, g: (i, u_blk0 + g)),
            pl.BlockSpec((BF16_SUBLANES, gc),
                         lambda i, g: (jnp.maximum(i * hb - 1, 0), u_blk0 + g)),
            pl.BlockSpec((BF16_SUBLANES, gc),
                         lambda i, g: (jnp.minimum((i + 1) * hb, n_hblk - 1), u_blk0 + g)),
            pl.BlockSpec((1, gc, oc), lambda i, g: (g, 0, 0)),
            pl.BlockSpec((1, oc), lambda i, g: (0, g)),
            pl.BlockSpec((tm, oc), lambda i, g: (i, ga_blk0 + g)),
            pl.BlockSpec((tm, oc), lambda i, g: (i, gp_blk0 + g)),
        ],
        out_specs=pl.BlockSpec((tm, oc), lambda i, g: (i, g)),
        out_shape=jax.ShapeDtypeStruct((m, d_model), BF16),
        scratch_shapes=[pltpu.VMEM((tm + 2 * POOL_HALO, gc), F32),
                        pltpu.VMEM((tm, gc), BF16)],
        compiler_params=pltpu.CompilerParams(
            dimension_semantics=("parallel", "arbitrary"),
            vmem_limit_bytes=_vmem_limit(blk, scratch_bytes=(tm + 16) * gc * 4 + tm * gc * 2,
                                         temp_bytes=6 * tm * oc * 4)),
        name="merge",
    )(attn_o, w_ap, proj, proj, proj, w_pool, pool_scale.reshape(1, d_model), proj, proj)


def _ffn_kernel(h_hbm, gn_ref, wg_ref, wv_ref, cwg_ref, cwv_ref, cbg_ref, cbv_ref, wd_ref,
                gf_ref, y_hbm, next_ref, acc_ref, rows_ref, edge_ref, row_sem, edge_sem, *,
                tm, tiles_per_seq, n_rows):
    i = pl.program_id(0)
    j = pl.program_id(1)
    ti = i % tiles_per_seq
    row0 = i * tm
    halo = BF16_SUBLANES
    n_chunks = tm // ROW_CHUNK
    d = acc_ref.shape[1]

    def chunk_rows(c):
        return pl.ds(pl.multiple_of(row0 + c * ROW_CHUNK, ROW_CHUNK), ROW_CHUNK)

    def rows_in(c, slot):
        return pltpu.make_async_copy(h_hbm.at[chunk_rows(c)], rows_ref.at[slot],
                                     row_sem.at[slot])

    def rows_out(c, slot):
        return pltpu.make_async_copy(rows_ref.at[slot], y_hbm.at[chunk_rows(c)],
                                     row_sem.at[slot])

    def edge_in(k):
        src = (jnp.maximum(row0 - F32_SUBLANES, 0) if k == 0
               else jnp.minimum(row0 + tm, n_rows - F32_SUBLANES))
        src = pl.multiple_of(src, F32_SUBLANES)
        return pltpu.make_async_copy(h_hbm.at[pl.ds(src, F32_SUBLANES)],
                                     edge_ref.at[pl.ds(k * F32_SUBLANES, F32_SUBLANES)],
                                     edge_sem.at[k])

    @pl.when(j == 0)
    def _():
        edge_in(0).start()
        edge_in(1).start()
        rows_in(0, 0).start()

        def load_pair(cp, carry):
            for slot in range(2):
                c = cp * 2 + slot

                @pl.when(c + 1 < n_chunks)
                def _():
                    rows_in(c + 1, 1 - slot).start()
                rows_in(c, slot).wait()
                x = rows_ref[slot]
                r = pl.multiple_of(c * ROW_CHUNK, ROW_CHUNK)
                next_ref[pl.ds(halo + r, ROW_CHUNK), :] = _rms_scale(x, gn_ref[...]).astype(BF16)
                acc_ref[pl.ds(r, ROW_CHUNK), :] = x
            return carry
        lax.fori_loop(0, n_chunks // 2, load_pair, 0)
        edge_in(0).wait()
        edge_in(1).wait()
        next_ref[0:halo, :] = _rms_scale(edge_ref[...], gn_ref[...]).astype(BF16)

    keep_prev = (ti != 0).astype(F32)
    keep_next = (ti != tiles_per_seq - 1).astype(F32)
    row = lax.broadcasted_iota(jnp.int32, (tm, 1), 0)

    def conv(w, cw, cb):
        up = jnp.dot(next_ref[...], w, preferred_element_type=F32)
        main = up[halo:]
        before = up[F32_SUBLANES - 1:F32_SUBLANES] * keep_prev
        after = up[F32_SUBLANES:F32_SUBLANES + 1] * keep_next
        dn = jnp.where(row == 0, before, pltpu.roll(main, 1, 0))
        nx = jnp.where(row == tm - 1, after, pltpu.roll(main, tm - 1, 0))
        return dn * cw[0:1, :] + main * cw[1:2, :] + nx * cw[2:3, :] + cb

    tc = wd_ref.shape[0]
    sub = min(FFN_SUB, tc)
    acts = []
    for s in range(tc // sub):
        cs = slice(s * sub, (s + 1) * sub)
        gate = conv(wg_ref[:, cs], cwg_ref[:, cs], cbg_ref[:, cs])
        val = conv(wv_ref[:, cs], cwv_ref[:, cs], cbv_ref[:, cs])
        acts.append((jax.nn.silu(gate) * val).astype(BF16))
    act = acts[0] if len(acts) == 1 else jnp.concatenate(acts, axis=1)
    nb = min(512, d)
    for n in range(d // nb):
        ns = slice(n * nb, (n + 1) * nb)
        acc_ref[:, ns] += jnp.dot(act, wd_ref[:, ns], preferred_element_type=F32)

    @pl.when(j == pl.num_programs(1) - 1)
    def _():
        def store_pair(cp, carry):
            for slot in range(2):
                c = cp * 2 + slot

                @pl.when(cp > 0)
                def _():
                    rows_out(c - 2, slot).wait()
                r = pl.multiple_of(c * ROW_CHUNK, ROW_CHUNK)
                rows_ref[slot] = _rms_scale(acc_ref[pl.ds(r, ROW_CHUNK), :], gf_ref[...])
                rows_out(c, slot).start()
            return carry
        lax.fori_loop(0, n_chunks // 2, store_pair, 0)
        rows_out(n_chunks - 2, 0).wait()
        rows_out(n_chunks - 1, 1).wait()


def _ffn(h, norm_g, w_gate, w_val, cw_gate, cw_val, cb_gate, cb_val, w_down, final_g, seq):
    m, d = h.shape
    d_ff = w_down.shape[0]
    tc = min(FFN_CHUNK, d_ff)
    assert d_ff % tc == 0
    nj = d_ff // tc
    tm = _pick_tile(seq, (1024, 512, 256, 128))
    assert tm % (2 * ROW_CHUNK) == 0
    tiles_per_seq = seq // tm
    blk = 2 * d * tc * 2 + tc * d * 2 + 2 * 3 * tc * 4 + 2 * tc * 4 + 2 * d * 4
    scratch = ((tm + BF16_SUBLANES) * d * 2 + tm * d * 4 + 2 * ROW_CHUNK * d * 4
               + 2 * F32_SUBLANES * d * 4)
    wspec = pl.BlockSpec((d, tc), lambda i, j: (0, j))
    cwspec = pl.BlockSpec((3, tc), lambda i, j: (0, j))
    cbspec = pl.BlockSpec((1, tc), lambda i, j: (0, j))
    gspec = pl.BlockSpec((1, d), lambda i, j: (0, 0))
    return pl.pallas_call(
        functools.partial(_ffn_kernel, tm=tm, tiles_per_seq=tiles_per_seq, n_rows=m),
        grid=(m // tm, nj),
        in_specs=[pl.BlockSpec(memory_space=pl.ANY), gspec, wspec, wspec, cwspec, cwspec,
                  cbspec, cbspec, pl.BlockSpec((tc, d), lambda i, j: (j, 0)), gspec],
        out_specs=pl.BlockSpec(memory_space=pl.ANY),
        out_shape=jax.ShapeDtypeStruct((m, d), F32),
        scratch_shapes=[pltpu.VMEM((tm + BF16_SUBLANES, d), BF16),
                        pltpu.VMEM((tm, d), F32),
                        pltpu.VMEM((2, ROW_CHUNK, d), F32),
                        pltpu.VMEM((2 * F32_SUBLANES, d), F32),
                        pltpu.SemaphoreType.DMA((2,)),
                        pltpu.SemaphoreType.DMA((2,))],
        compiler_params=pltpu.CompilerParams(
            dimension_semantics=("arbitrary", "arbitrary"),
            vmem_limit_bytes=_vmem_limit(blk, scratch_bytes=scratch,
                                         temp_bytes=10 * tm * FFN_SUB * 4)),
        name="ffn",
    )(h, norm_g.reshape(1, d), w_gate, w_val, cw_gate, cw_val, cb_gate, cb_val, w_down,
      final_g.reshape(1, d))


def _pad_cols(x, width):
    return jnp.pad(x, ((0, 0), (0, width - x.shape[1])))


def _trunk(x, layers, final_g):
    batch, seq, d_model = x.shape
    h = x.reshape(batch * seq, d_model)
    n_layers = len(layers)
    for li, p in enumerate(layers):
        q_width = p["w_attn_proj"].shape[0]
        n_groups, gc, _ = p["w_pool"].shape
        pool_width = n_groups * gc
        in_width = p["w_in"].shape[1]
        kv_width = (in_width - q_width - pool_width - 2 * d_model) // 2
        n_q_heads, n_kv_heads = q_width // HEAD_DIM, kv_width // HEAD_DIM
        v_off = q_width + kv_width
        u_off = q_width + 2 * kv_width
        ga_off = u_off + pool_width
        gp_off = ga_off + d_model

        n = _rmsnorm(h, p["norm_mix_g"], BF16)
        proj = _matmul(n, p["w_in"], BF16, name="in_proj")
        q, kt = _qk_prep(proj, batch, seq, n_q_heads, n_kv_heads, p["q_norm_g"], p["k_norm_g"])
        attn_o = _attention(q, kt, proj, batch, seq, n_q_heads, n_kv_heads, v_off)
        merged = _merge(attn_o, proj, p["w_attn_proj"], p["w_pool"], p["pool_scale"], seq,
                        u_off, ga_off, gp_off)
        h = _matmul(merged, p["w_out"], F32, residual=h, name="out_proj")
        assert li == n_layers - 1, "only the last layer's FFN applies the final norm"
        h = _ffn(h, p["norm_ffn_g"], p["w_gate"], p["w_val"], p["cw_gate"], p["cw_val"],
                 p["cb_gate"], p["cb_val"], p["w_down"], final_g, seq)
    return h.reshape(batch, seq, d_model)


def kernel(x_prompt, x_sample, norm_mix_g, w_in, q_norm_g, k_norm_g, w_attn_proj, w_pool,
           pool_scale, w_out, norm_ffn_g, w_up, conv_w, conv_b, w_down, norm_final_g):
    depth = w_in.shape[0]
    assert depth == 1, "the fused FFN epilogue assumes a single layer"
    d_ff = w_down.shape[1]
    tc = min(FFN_CHUNK, d_ff)
    d_ff_pad = -(-d_ff // tc) * tc
    layers = []
    for l in range(depth):
        layers.append({
            "norm_mix_g": norm_mix_g[l], "w_in": w_in[l].astype(BF16),
            "q_norm_g": q_norm_g[l], "k_norm_g": k_norm_g[l],
            "w_attn_proj": w_attn_proj[l].astype(BF16), "w_pool": w_pool[l].astype(BF16),
            "pool_scale": pool_scale[l], "w_out": w_out[l].astype(BF16),
            "norm_ffn_g": norm_ffn_g[l],
            "w_gate": _pad_cols(w_up[l][:, :d_ff].astype(BF16), d_ff_pad),
            "w_val": _pad_cols(w_up[l][:, d_ff:].astype(BF16), d_ff_pad),
            "cw_gate": _pad_cols(conv_w[l][:, :d_ff], d_ff_pad),
            "cw_val": _pad_cols(conv_w[l][:, d_ff:], d_ff_pad),
            "cb_gate": _pad_cols(conv_b[l][None, :d_ff], d_ff_pad),
            "cb_val": _pad_cols(conv_b[l][None, d_ff:], d_ff_pad),
            "w_down": jnp.pad(w_down[l].astype(BF16), ((0, d_ff_pad - d_ff), (0, 0))),
        })
    return (_trunk(x_prompt, layers, norm_final_g), _trunk(x_sample, layers, norm_final_g))
```

```python
import functools

import jax
import jax.numpy as jnp
from jax import lax
from jax.experimental import pallas as pl
from jax.experimental.pallas import tpu as pltpu

EPS = 1e-6
HEAD_DIM = 128
ROPE_THETA = 10000.0
GRID_W = 64
POOL_WINDOWS = (2, 4, 8, 16)
POOL_HALO = 8
F32_SUBLANES = 8
BF16_SUBLANES = 16
ROW_CHUNK = 32
FFN_CHUNK = 512
FFN_SUB = 256
V7X_VMEM_BYTES = 64 * 1024 * 1024
VMEM_RESERVE_BYTES = 2 * 1024 * 1024

BF16 = jnp.bfloat16
F32 = jnp.float32


def _vmem_limit(block_bytes, scratch_bytes=0, temp_bytes=0):
    want = 2 * block_bytes + scratch_bytes + temp_bytes + (4 << 20)
    return int(min(want, V7X_VMEM_BYTES - VMEM_RESERVE_BYTES))


def _pick_tile(dim, candidates):
    for c in candidates:
        if dim % c == 0:
            return c
    return dim


def _rms_scale(x, g):
    inv = lax.rsqrt(jnp.mean(x * x, axis=-1, keepdims=True) + EPS)
    return x * inv * g


def _rmsnorm_kernel(x_ref, g_ref, o_ref):
    o_ref[...] = _rms_scale(x_ref[...], g_ref[...]).astype(o_ref.dtype)


def _rmsnorm(x, g, out_dtype):
    m, d = x.shape
    tr = _pick_tile(m, (256, 128, 64, 32, 16))
    blk = tr * d * (4 + jnp.dtype(out_dtype).itemsize)
    return pl.pallas_call(
        _rmsnorm_kernel,
        grid=(m // tr,),
        in_specs=[pl.BlockSpec((tr, d), lambda i: (i, 0)),
                  pl.BlockSpec((1, d), lambda i: (0, 0))],
        out_specs=pl.BlockSpec((tr, d), lambda i: (i, 0)),
        out_shape=jax.ShapeDtypeStruct((m, d), out_dtype),
        compiler_params=pltpu.CompilerParams(
            dimension_semantics=("parallel",),
            vmem_limit_bytes=_vmem_limit(blk, temp_bytes=tr * d * 8)),
        name="rmsnorm",
    )(x, g.reshape(1, d))


def _matmul_kernel(a_ref, b_ref, o_ref):
    o_ref[...] = jnp.dot(a_ref[...], b_ref[...],
                         preferred_element_type=F32).astype(o_ref.dtype)


def _matmul_res_kernel(a_ref, b_ref, r_ref, o_ref):
    o_ref[...] = r_ref[...] + jnp.dot(a_ref[...], b_ref[...],
                                      preferred_element_type=F32)


def _matmul(a, b, out_dtype, residual=None, name="matmul"):
    m, k = a.shape
    _, n = b.shape
    tm = _pick_tile(m, (1024, 512, 256, 128))
    tn = _pick_tile(n, (1024, 512, 256, 128))
    osz = jnp.dtype(out_dtype).itemsize
    blk = tm * k * 2 + k * tn * 2 + tm * tn * osz
    in_specs = [pl.BlockSpec((tm, k), lambda i, j: (i, 0)),
                pl.BlockSpec((k, tn), lambda i, j: (0, j))]
    args = [a, b]
    body = _matmul_kernel
    if residual is not None:
        in_specs.append(pl.BlockSpec((tm, tn), lambda i, j: (i, j)))
        args.append(residual)
        blk += tm * tn * 4
        body = _matmul_res_kernel
    return pl.pallas_call(
        body,
        grid=(m // tm, n // tn),
        in_specs=in_specs,
        out_specs=pl.BlockSpec((tm, tn), lambda i, j: (i, j)),
        out_shape=jax.ShapeDtypeStruct((m, n), out_dtype),
        compiler_params=pltpu.CompilerParams(
            dimension_semantics=("parallel", "parallel"),
            vmem_limit_bytes=_vmem_limit(blk, temp_bytes=tm * tn * 4)),
        name=name,
    )(*args)


def _rope_tables(s):
    rows = s // GRID_W
    half = HEAD_DIM // 2
    quarter = half // 2
    inv_freq = ROPE_THETA ** (-jnp.arange(0, half, 2, dtype=F32) / half)
    row = jnp.repeat(jnp.arange(rows, dtype=F32), GRID_W)
    col = jnp.tile(jnp.arange(GRID_W, dtype=F32), rows)
    ang_r = row[:, None] * inv_freq[None, :]
    ang_c = col[:, None] * inv_freq[None, :]
    ang = jnp.concatenate([ang_r, ang_r, ang_c, ang_c], axis=-1)
    cos, sin = jnp.cos(ang), jnp.sin(ang)
    first = (jnp.arange(HEAD_DIM) % half) < quarter
    sin_a = jnp.where(first[None, :], -sin, 0.0)
    sin_b = jnp.where(first[None, :], 0.0, sin)
    return cos, sin_a, sin_b


def _norm_rope(x, g, cos, sin_a, sin_b):
    quarter = HEAD_DIM // 4
    xn = _rms_scale(x, g)
    return (xn * cos + pltpu.roll(xn, HEAD_DIM - quarter, 1) * sin_a
            + pltpu.roll(xn, quarter, 1) * sin_b)


def _qk_prep_kernel(x_ref, cos_ref, sa_ref, sb_ref, qg_ref, kg_ref, q_ref, kt_ref, *,
                    n_q_heads, n_kv_heads, scale):
    cos, sa, sb = cos_ref[...], sa_ref[...], sb_ref[...]
    for h in range(n_q_heads):
        x = x_ref[:, h * HEAD_DIM:(h + 1) * HEAD_DIM].astype(F32)
        q_ref[:, h * HEAD_DIM:(h + 1) * HEAD_DIM] = (
            _norm_rope(x, qg_ref[...], cos, sa, sb) * scale).astype(BF16)
    for h in range(n_kv_heads):
        c0 = (n_q_heads + h) * HEAD_DIM
        x = x_ref[:, c0:c0 + HEAD_DIM].astype(F32)
        kt_ref[h] = _norm_rope(x, kg_ref[...], cos, sa, sb).T.astype(BF16)


def _qk_prep(proj, batch, seq, n_q_heads, n_kv_heads, q_norm_g, k_norm_g):
    m = batch * seq
    q_width = n_q_heads * HEAD_DIM
    qk_width = (n_q_heads + n_kv_heads) * HEAD_DIM
    tr = _pick_tile(seq, (256, 128))
    nt = seq // tr
    cos, sin_a, sin_b = _rope_tables(seq)
    tab = pl.BlockSpec((tr, HEAD_DIM), lambda i: (i % nt, 0))
    gspec = pl.BlockSpec((1, HEAD_DIM), lambda i: (0, 0))
    blk = tr * qk_width * 2 * 2 + 3 * tr * HEAD_DIM * 4
    return pl.pallas_call(
        functools.partial(_qk_prep_kernel, n_q_heads=n_q_heads, n_kv_heads=n_kv_heads,
                          scale=HEAD_DIM ** -0.5),
        grid=(m // tr,),
        in_specs=[pl.BlockSpec((tr, qk_width), lambda i: (i, 0)), tab, tab, tab, gspec, gspec],
        out_specs=[pl.BlockSpec((tr, q_width), lambda i: (i, 0)),
                   pl.BlockSpec((n_kv_heads, HEAD_DIM, tr), lambda i: (i // nt, 0, i % nt))],
        out_shape=[jax.ShapeDtypeStruct((m, q_width), BF16),
                   jax.ShapeDtypeStruct((batch * n_kv_heads, HEAD_DIM, seq), BF16)],
        compiler_params=pltpu.CompilerParams(
            dimension_semantics=("parallel",),
            vmem_limit_bytes=_vmem_limit(blk, temp_bytes=16 << 20)),
        name="qk_prep",
    )(proj, cos, sin_a, sin_b, q_norm_g.reshape(1, HEAD_DIM), k_norm_g.reshape(1, HEAD_DIM))


def _attn_kernel(q_ref, kt_ref, v_ref, o_ref, *, tq, tk, seq, group):
    q = jnp.concatenate([q_ref[:, g * HEAD_DIM:(g + 1) * HEAD_DIM] for g in range(group)],
                        axis=0)
    rows = group * tq
    m = jnp.full((rows, 1), -jnp.inf, F32)
    l = jnp.zeros((rows, 1), F32)
    acc = jnp.zeros((rows, HEAD_DIM), F32)
    for c in range(seq // tk):
        s = jnp.dot(q, kt_ref[0, :, c * tk:(c + 1) * tk], preferred_element_type=F32)
        m_new = jnp.maximum(m, jnp.max(s, axis=-1, keepdims=True))
        alpha = jnp.exp(m - m_new)
        p = jnp.exp(s - m_new)
        l = alpha * l + jnp.sum(p, axis=-1, keepdims=True)
        acc = alpha * acc + jnp.dot(p.astype(BF16), v_ref[c * tk:(c + 1) * tk, :],
                                    preferred_element_type=F32)
        m = m_new
    o = acc / l
    for g in range(group):
        o_ref[:, g * HEAD_DIM:(g + 1) * HEAD_DIM] = o[g * tq:(g + 1) * tq].astype(o_ref.dtype)


def _attention(q, kt, proj, batch, seq, n_q_heads, n_kv_heads, v_off):
    m = batch * seq
    group = n_q_heads // n_kv_heads
    q_width = n_q_heads * HEAD_DIM
    tq = _pick_tile(seq, (256, 128))
    tk = min(512, seq)
    nq = seq // tq
    gw = group * HEAD_DIM
    v_blk0 = v_off // HEAD_DIM
    blk = tq * gw * 2 * 2 + 2 * seq * HEAD_DIM * 2
    return pl.pallas_call(
        functools.partial(_attn_kernel, tq=tq, tk=tk, seq=seq, group=group),
        grid=(batch, n_kv_heads, nq),
        in_specs=[
            pl.BlockSpec((tq, gw), lambda b, h, i: (b * nq + i, h)),
            pl.BlockSpec((1, HEAD_DIM, seq), lambda b, h, i: (b * n_kv_heads + h, 0, 0)),
            pl.BlockSpec((seq, HEAD_DIM), lambda b, h, i: (b, v_blk0 + h)),
        ],
        out_specs=pl.BlockSpec((tq, gw), lambda b, h, i: (b * nq + i, h)),
        out_shape=jax.ShapeDtypeStruct((m, q_width), BF16),
        compiler_params=pltpu.CompilerParams(
            dimension_semantics=("parallel", "parallel", "parallel"),
            vmem_limit_bytes=_vmem_limit(blk, temp_bytes=10 * group * tq * tk * 4)),
        name="attention",
    )(q, kt, proj)


def _merge_kernel(ao_ref, wap_ref, u_ref, uprev_ref, unext_ref, wp_ref, ps_ref, ga_ref, gp_ref,
                  o_ref, uext_ref, d_ref, *, tm, seq, tiles_per_seq):
    i = pl.program_id(0)
    g = pl.program_id(1)
    ti = i % tiles_per_seq
    keep_prev = (ti != 0).astype(F32)
    keep_next = (ti != tiles_per_seq - 1).astype(F32)
    uext_ref[0:POOL_HALO, :] = uprev_ref[...].astype(F32)[BF16_SUBLANES - POOL_HALO:] * keep_prev
    uext_ref[POOL_HALO:POOL_HALO + tm, :] = u_ref[...].astype(F32)
    uext_ref[POOL_HALO + tm:, :] = unext_ref[...].astype(F32)[:POOL_HALO] * keep_next

    t = ti * tm + lax.broadcasted_iota(jnp.int32, (tm, 1), 0)
    for gg, w in enumerate(POOL_WINDOWS):
        @pl.when(g == gg)
        def _(w=w):
            lo_off = -(w // 2)
            hi_off = w - w // 2
            wsum = uext_ref[POOL_HALO + lo_off:POOL_HALO + lo_off + tm, :]
            for off in range(lo_off + 1, hi_off):
                wsum = wsum + uext_ref[POOL_HALO + off:POOL_HALO + off + tm, :]
            count = (jnp.minimum(t + hi_off, seq) - jnp.maximum(t + lo_off, 0)).astype(F32)
            d = wsum / count - uext_ref[POOL_HALO:POOL_HALO + tm, :]
            d_ref[...] = d.astype(BF16)

    attn = jnp.dot(ao_ref[...], wap_ref[...], preferred_element_type=F32)
    pool = jnp.dot(d_ref[...], wp_ref[0], preferred_element_type=F32) * ps_ref[...]
    merged = (jax.nn.sigmoid(ga_ref[...].astype(F32)) * attn
              + jax.nn.sigmoid(gp_ref[...].astype(F32)) * pool)
    o_ref[...] = merged.astype(o_ref.dtype)


def _merge(attn_o, proj, w_ap, w_pool, pool_scale, seq, u_off, ga_off, gp_off):
    m, q_width = attn_o.shape
    n_groups, gc, oc = w_pool.shape
    d_model = n_groups * oc
    assert n_groups == len(POOL_WINDOWS)
    tm = _pick_tile(seq, (512, 256, 128))
    tiles_per_seq = seq // tm
    u_blk0, ga_blk0, gp_blk0 = u_off // gc, ga_off // oc, gp_off // oc
    assert u_off % gc == 0 and ga_off % oc == 0 and gp_off % oc == 0
    hb = tm // BF16_SUBLANES
    n_hblk = m // BF16_SUBLANES
    blk = (tm * q_width * 2 + q_width * oc * 2 + tm * gc * 2 + 2 * BF16_SUBLANES * gc * 2
           + gc * oc * 2 + oc * 4 + 3 * tm * oc * 2)
    return pl.pallas_call(
        functools.partial(_merge_kernel, tm=tm, seq=seq, tiles_per_seq=tiles_per_seq),
        grid=(m // tm, n_groups),
        in_specs=[
            pl.BlockSpec((tm, q_width), lambda i, g: (i, 0)),
            pl.BlockSpec((q_width, oc), lambda i, g: (0, g)),
            pl.BlockSpec((tm, gc), lambda i, g: (i, u_blk0 + g)),
            pl.BlockSpec((BF16_SUBLANES, gc),
                         lambda i, g: (jnp.maximum(i * hb - 1, 0), u_blk0 + g)),
            pl.BlockSpec((BF16_SUBLANES, gc),
                         lambda i, g: (jnp.minimum((i + 1) * hb, n_hblk - 1), u_blk0 + g)),
            pl.BlockSpec((1, gc, oc), lambda i, g: (g, 0, 0)),
            pl.BlockSpec((1, oc), lambda i, g: (0, g)),
            pl.BlockSpec((tm, oc), lambda i, g: (i, ga_blk0 + g)),
            pl.BlockSpec((tm, oc), lambda i, g: (i, gp_blk0 + g)),
        ],
        out_specs=pl.BlockSpec((tm, oc), lambda i, g: (i, g)),
        out_shape=jax.ShapeDtypeStruct((m, d_model), BF16),
        scratch_shapes=[pltpu.VMEM((tm + 2 * POOL_HALO, gc), F32),
                        pltpu.VMEM((tm, gc), BF16)],
        compiler_params=pltpu.CompilerParams(
            dimension_semantics=("parallel", "arbitrary"),
            vmem_limit_bytes=_vmem_limit(blk, scratch_bytes=(tm + 16) * gc * 4 + tm * gc * 2,
                                         temp_bytes=6 * tm * oc * 4)),
        name="merge",
    )(attn_o, w_ap, proj, proj, proj, w_pool, pool_scale.reshape(1, d_model), proj, proj)


def _ffn_kernel(h_hbm, gn_ref, wg_ref, wv_ref, cwg_ref, cwv_ref, cbg_ref, cbv_ref, wd_ref,
                gf_ref, y_hbm, next_ref, acc_ref, rows_ref, edge_ref, row_sem, edge_sem, *,
                tm, tiles_per_seq, n_rows):
    i = pl.program_id(0)
    j = pl.program_id(1)
    ti = i % tiles_per_seq
    row0 = i * tm
    halo = BF16_SUBLANES
    n_chunks = tm // ROW_CHUNK
    d = acc_ref.shape[1]

    def chunk_rows(c):
        return pl.ds(pl.multiple_of(row0 + c * ROW_CHUNK, ROW_CHUNK), ROW_CHUNK)

    def rows_in(c, slot):
        return pltpu.make_async_copy(h_hbm.at[chunk_rows(c)], rows_ref.at[slot],
                                     row_sem.at[slot])

    def rows_out(c, slot):
        return pltpu.make_async_copy(rows_ref.at[slot], y_hbm.at[chunk_rows(c)],
                                     row_sem.at[slot])

    def edge_in(k):
        src = (jnp.maximum(row0 - F32_SUBLANES, 0) if k == 0
               else jnp.minimum(row0 + tm, n_rows - F32_SUBLANES))
        src = pl.multiple_of(src, F32_SUBLANES)
        return pltpu.make_async_copy(h_hbm.at[pl.ds(src, F32_SUBLANES)],
                                     edge_ref.at[pl.ds(k * F32_SUBLANES, F32_SUBLANES)],
                                     edge_sem.at[k])

    @pl.when(j == 0)
    def _():
        edge_in(0).start()
        edge_in(1).start()
        rows_in(0, 0).start()

        def load_pair(cp, carry):
            for slot in range(2):
                c = cp * 2 + slot

                @pl.when(c + 1 < n_chunks)
                def _():
                    rows_in(c + 1, 1 - slot).start()
                rows_in(c, slot).wait()
                x = rows_ref[slot]
                r = pl.multiple_of(c * ROW_CHUNK, ROW_CHUNK)
                next_ref[pl.ds(halo + r, ROW_CHUNK), :] = _rms_scale(x, gn_ref[...]).astype(BF16)
                acc_ref[pl.ds(r, ROW_CHUNK), :] = x
            return carry
        lax.fori_loop(0, n_chunks // 2, load_pair, 0)
        edge_in(0).wait()
        edge_in(1).wait()
        next_ref[0:halo, :] = _rms_scale(edge_ref[...], gn_ref[...]).astype(BF16)

    keep_prev = (ti != 0).astype(F32)
    keep_next = (ti != tiles_per_seq - 1).astype(F32)
    row = lax.broadcasted_iota(jnp.int32, (tm, 1), 0)

    def conv(w, cw, cb):
        up = jnp.dot(next_ref[...], w, preferred_element_type=F32)
        main = up[halo:]
        before = up[F32_SUBLANES - 1:F32_SUBLANES] * keep_prev
        after = up[F32_SUBLANES:F32_SUBLANES + 1] * keep_next
        dn = jnp.where(row == 0, before, pltpu.roll(main, 1, 0))
        nx = jnp.where(row == tm - 1, after, pltpu.roll(main, tm - 1, 0))
        return dn * cw[0:1, :] + main * cw[1:2, :] + nx * cw[2:3, :] + cb

    tc = wd_ref.shape[0]
    sub = min(FFN_SUB, tc)
    acts = []
    for s in range(tc // sub):
        cs = slice(s * sub, (s + 1) * sub)
        gate = conv(wg_ref[:, cs], cwg_ref[:, cs], cbg_ref[:, cs])
        val = conv(wv_ref[:, cs], cwv_ref[:, cs], cbv_ref[:, cs])
        acts.append((jax.nn.silu(gate) * val).astype(BF16))
    act = acts[0] if len(acts) == 1 else jnp.concatenate(acts, axis=1)
    nb = min(512, d)
    for n in range(d // nb):
        ns = slice(n * nb, (n + 1) * nb)
        acc_ref[:, ns] += jnp.dot(act, wd_ref[:, ns], preferred_element_type=F32)

    @pl.when(j == pl.num_programs(1) - 1)
    def _():
        def store_pair(cp, carry):
            for slot in range(2):
                c = cp * 2 + slot

                @pl.when(cp > 0)
                def _():
                    rows_out(c - 2, slot).wait()
                r = pl.multiple_of(c * ROW_CHUNK, ROW_CHUNK)
                rows_ref[slot] = _rms_scale(acc_ref[pl.ds(r, ROW_CHUNK), :], gf_ref[...])
                rows_out(c, slot).start()
            return carry
        lax.fori_loop(0, n_chunks // 2, store_pair, 0)
        rows_out(n_chunks - 2, 0).wait()
        rows_out(n_chunks - 1, 1).wait()


def _ffn(h, norm_g, w_up, conv_w, conv_b, w_down, final_g, seq):
    m, d = h.shape
    d_ff = w_down.shape[0]
    tc = min(FFN_CHUNK, d_ff)
    assert d_ff % tc == 0
    nj = d_ff // tc
    tm = _pick_tile(seq, (1024, 512, 256, 128))
    assert tm % (2 * ROW_CHUNK) == 0
    tiles_per_seq = seq // tm
    blk = 2 * d * tc * 2 + tc * d * 2 + 2 * 3 * tc * 4 + 2 * tc * 4 + 2 * d * 4
    scratch = ((tm + BF16_SUBLANES) * d * 2 + tm * d * 4 + 2 * ROW_CHUNK * d * 4
               + 2 * F32_SUBLANES * d * 4)
    def halves(rows):
        return (pl.BlockSpec((rows, tc), lambda i, j: (0, j)),
                pl.BlockSpec((rows, tc), lambda i, j: (0, nj + j)))
    gspec = pl.BlockSpec((1, d), lambda i, j: (0, 0))
    return pl.pallas_call(
        functools.partial(_ffn_kernel, tm=tm, tiles_per_seq=tiles_per_seq, n_rows=m),
        grid=(m // tm, nj),
        in_specs=[pl.BlockSpec(memory_space=pl.ANY), gspec, *halves(d), *halves(3), *halves(1),
                  pl.BlockSpec((tc, d), lambda i, j: (j, 0)), gspec],
        out_specs=pl.BlockSpec(memory_space=pl.ANY),
        out_shape=jax.ShapeDtypeStruct((m, d), F32),
        scratch_shapes=[pltpu.VMEM((tm + BF16_SUBLANES, d), BF16),
                        pltpu.VMEM((tm, d), F32),
                        pltpu.VMEM((2, ROW_CHUNK, d), F32),
                        pltpu.VMEM((2 * F32_SUBLANES, d), F32),
                        pltpu.SemaphoreType.DMA((2,)),
                        pltpu.SemaphoreType.DMA((2,))],
        compiler_params=pltpu.CompilerParams(
            dimension_semantics=("arbitrary", "arbitrary"),
            vmem_limit_bytes=_vmem_limit(blk, scratch_bytes=scratch,
                                         temp_bytes=10 * tm * FFN_SUB * 4)),
        name="ffn",
    )(h, norm_g.reshape(1, d), w_up, w_up, conv_w, conv_w, conv_b, conv_b, w_down,
      final_g.reshape(1, d))


def _pad_halves(x, half, half_pad):
    z = jnp.zeros((x.shape[0], half_pad - half), x.dtype)
    return jnp.concatenate([x[:, :half], z, x[:, half:], z], axis=1)


def _trunk(x, layers, final_g):
    batch, seq, d_model = x.shape
    h = x.reshape(batch * seq, d_model)
    n_layers = len(layers)
    for li, p in enumerate(layers):
        q_width = p["w_attn_proj"].shape[0]
        n_groups, gc, _ = p["w_pool"].shape
        pool_width = n_groups * gc
        in_width = p["w_in"].shape[1]
        kv_width = (in_width - q_width - pool_width - 2 * d_model) // 2
        n_q_heads, n_kv_heads = q_width // HEAD_DIM, kv_width // HEAD_DIM
        v_off = q_width + kv_width
        u_off = q_width + 2 * kv_width
        ga_off = u_off + pool_width
        gp_off = ga_off + d_model

        n = _rmsnorm(h, p["norm_mix_g"], BF16)
        proj = _matmul(n, p["w_in"], BF16, name="in_proj")
        q, kt = _qk_prep(proj, batch, seq, n_q_heads, n_kv_heads, p["q_norm_g"], p["k_norm_g"])
        attn_o = _attention(q, kt, proj, batch, seq, n_q_heads, n_kv_heads, v_off)
        merged = _merge(attn_o, proj, p["w_attn_proj"], p["w_pool"], p["pool_scale"], seq,
                        u_off, ga_off, gp_off)
        h = _matmul(merged, p["w_out"], F32, residual=h, name="out_proj")
        assert li == n_layers - 1, "only the last layer's FFN applies the final norm"
        h = _ffn(h, p["norm_ffn_g"], p["w_up"], p["conv_w"], p["conv_b"], p["w_down"],
                 final_g, seq)
    return h.reshape(batch, seq, d_model)


def kernel(x_prompt, x_sample, norm_mix_g, w_in, q_norm_g, k_norm_g, w_attn_proj, w_pool,
           pool_scale, w_out, norm_ffn_g, w_up, conv_w, conv_b, w_down, norm_final_g):
    depth = w_in.shape[0]
    assert depth == 1, "the fused FFN epilogue assumes a single layer"
    d_ff = w_down.shape[1]
    tc = min(FFN_CHUNK, d_ff)
    d_ff_pad = -(-d_ff // tc) * tc
    layers = []
    for l in range(depth):
        layers.append({
            "norm_mix_g": norm_mix_g[l], "w_in": w_in[l].astype(BF16),
            "q_norm_g": q_norm_g[l], "k_norm_g": k_norm_g[l],
            "w_attn_proj": w_attn_proj[l].astype(BF16), "w_pool": w_pool[l].astype(BF16),
            "pool_scale": pool_scale[l], "w_out": w_out[l].astype(BF16),
            "norm_ffn_g": norm_ffn_g[l],
            "w_up": _pad_halves(w_up[l].astype(BF16), d_ff, d_ff_pad),
            "conv_w": _pad_halves(conv_w[l], d_ff, d_ff_pad),
            "conv_b": _pad_halves(conv_b[l][None, :], d_ff, d_ff_pad),
            "w_down": jnp.pad(w_down[l].astype(BF16), ((0, d_ff_pad - d_ff), (0, 0))),
        })
    return (_trunk(x_prompt, layers, norm_final_g), _trunk(x_sample, layers, norm_final_g))
```
